```python
import math
import jax
import jax.numpy as jnp
from jax import lax
import numpy as np

D_MODEL = 2048
BATCH = 8
SEQ = 4096
DEPTH = 2

CTX_LEN = 256
GRID_W = 64
Q_BLOCK = 128
ROPE_THETA = 10000.0
NORM_EPS = 1e-6
N_MOD = 9
D_FF = 5632
BRANCH_W = 1024
N_BRANCH = 3

MLA_HEADS = 8
MLA_Q_LORA = 1536
MLA_KV_LORA = 512
MLA_NOPE = 128
MLA_ROPE = 64
MLA_V = 128
MLA_SCALE = (MLA_NOPE + MLA_ROPE) ** -0.5

S5_W = 1024
S5_H = 16
S5_G = S5_W // S5_H
S5_P = 64
S5_DT_MIN = 1e-3
S5_DT_MAX = 1e-1
S5_LAMBDA_RE_MAX = -1e-4

GQA_Q_HEADS = 8
GQA_KV_HEADS = 2
GQA_HEAD_DIM = 128
GQA_SCALE = GQA_HEAD_DIM ** -0.5

IN_WIDTHS = (MLA_Q_LORA, MLA_KV_LORA, MLA_ROPE, S5_W,
             GQA_Q_HEADS * GQA_HEAD_DIM, GQA_KV_HEADS * GQA_HEAD_DIM, GQA_KV_HEADS * GQA_HEAD_DIM)
D_IN = sum(IN_WIDTHS)
IN_OFFSETS = tuple(int(v) for v in np.cumsum(IN_WIDTHS)[:-1])

kernel_name = 'hybrid_mla_s5_gqa_macaron_dit'


def rms_norm(x, g):
    xf = x.astype(jnp.float32)
    y = xf * lax.rsqrt(jnp.mean(xf * xf, axis=-1, keepdims=True) + NORM_EPS)
    return (y * g.astype(jnp.float32)).astype(x.dtype)


def modulate(h, shift, scale):
    return h * (1 + scale) + shift


def swiglu(h, w_up, w_down):
    gate, up = jnp.split(h @ w_up, 2, axis=-1)
    return (jax.nn.silu(gate) * up) @ w_down


def grid_positions(n_tokens):
    rows = n_tokens // GRID_W
    r, col = jnp.meshgrid(jnp.arange(rows, dtype=jnp.int32), jnp.arange(GRID_W, dtype=jnp.int32), indexing='ij')
    return r.reshape(-1), col.reshape(-1)


def axial_rope_tables(rows, cols, d_rot):
    half = d_rot // 2
    freqs = ROPE_THETA ** (-jnp.arange(0, half, 2, dtype=jnp.float32) / half)

    def table(pos):
        ang = pos.astype(jnp.float32)[:, None] * freqs[None, :]
        ang = jnp.concatenate([ang, ang], axis=-1)
        return jnp.cos(ang), jnp.sin(ang)

    cr, sr = table(rows)
    cc, sc = table(cols)
    return jnp.concatenate([cr, cc], axis=-1), jnp.concatenate([sr, sc], axis=-1)


def rotate_half(v):
    v1, v2 = jnp.split(v, 2, axis=-1)
    return jnp.concatenate([-v2, v1], axis=-1)


def apply_axial_rope(x, rope):
    cos, sin = rope
    half = x.shape[-1] // 2
    xf = x.astype(jnp.float32)
    rotated = jnp.concatenate([rotate_half(xf[..., :half]), rotate_half(xf[..., half:])], axis=-1)
    return (xf * cos[:, None, :] + rotated * sin[:, None, :]).astype(x.dtype)


def blocked_attention(q, k, v, scale):
    b, lq, hq, dk = q.shape
    hkv, dv = k.shape[2], v.shape[-1]
    grp = hq // hkv
    nb = lq // Q_BLOCK
    qb = jnp.moveaxis(q.reshape(b, nb, Q_BLOCK, hkv, grp, dk), 1, 0)

    def one_block(q_blk):
        s = jnp.einsum('bqkgd,bskd->bkgqs', q_blk, k).astype(jnp.float32) * scale
        pr = jax.nn.softmax(s, axis=-1).astype(v.dtype)
        return jnp.einsum('bkgqs,bskd->bqkgd', pr, v)

    o = lax.map(one_block, qb)
    return jnp.moveaxis(o, 0, 1).reshape(b, lq, hq * dv)


def mla_q(cq, p, rope):
    b, l = cq.shape[:2]
    q = (rms_norm(cq, p['g_cq']) @ p['w_uq']).reshape(b, l, MLA_HEADS, MLA_NOPE + MLA_ROPE)
    q_nope, q_rope = q[..., :MLA_NOPE], q[..., MLA_NOPE:]
    if rope is not None:
        q_rope = apply_axial_rope(q_rope, rope)
    return jnp.concatenate([q_nope, q_rope], axis=-1)


def mla_kv(ckv, kr, p, rope):
    b, l = ckv.shape[:2]
    kv = (rms_norm(ckv, p['g_ckv']) @ p['w_ukv']).reshape(b, l, MLA_HEADS, MLA_NOPE + MLA_V)
    k_nope, v = kv[..., :MLA_NOPE], kv[..., MLA_NOPE:]
    k_rope = kr[:, :, None, :]
    if rope is not None:
        k_rope = apply_axial_rope(k_rope, rope)
    k_rope = jnp.broadcast_to(k_rope, (b, l, MLA_HEADS, MLA_ROPE))
    return jnp.concatenate([k_nope, k_rope], axis=-1), v


def gqa_q(gq, p, rope):
    b, l = gq.shape[:2]
    q = rms_norm(gq.reshape(b, l, GQA_Q_HEADS, GQA_HEAD_DIM), p['g_q'])
    return apply_axial_rope(q, rope) if rope is not None else q


def gqa_kv(gk, gv, p, rope):
    b, l = gk.shape[:2]
    k = rms_norm(gk.reshape(b, l, GQA_KV_HEADS, GQA_HEAD_DIM), p['g_k'])
    if rope is not None:
        k = apply_axial_rope(k, rope)
    return k, gv.reshape(b, l, GQA_KV_HEADS, GQA_HEAD_DIM)


def s5_discretize(lam_re, lam_im, log_dt, b_re, b_im):
    lr = jnp.minimum(lam_re.astype(jnp.float32), S5_LAMBDA_RE_MAX)
    li = lam_im.astype(jnp.float32)
    dt = jnp.exp(log_dt.astype(jnp.float32))[:, None]
    mag = jnp.exp(lr * dt)
    a_r, a_i = mag * jnp.cos(li * dt), mag * jnp.sin(li * dt)
    den = lr * lr + li * li
    n_r, n_i = a_r - 1.0, a_i
    f_r = (n_r * lr + n_i * li) / den
    f_i = (n_i * lr - n_r * li) / den
    br, bi = b_re.astype(jnp.float32), b_im.astype(jnp.float32)
    bb_r = f_r[..., None] * br - f_i[..., None] * bi
    bb_i = f_r[..., None] * bi + f_i[..., None] * br
    return a_r, a_i, bb_r, bb_i


def complex_affine_combine(e1, e2):
    a1r, a1i, b1r, b1i = e1
    a2r, a2i, b2r, b2i = e2
    return (a2r * a1r - a2i * a1i,
            a2r * a1i + a2i * a1r,
            a2r * b1r - a2i * b1i + b2r,
            a2r * b1i + a2i * b1r + b2i)


def s5_scan(u, disc, h0, reverse):
    a_r, a_i, bb_r, bb_i = disc
    bu_r = jnp.einsum('blgh,gph->blgp', u, bb_r)
    bu_i = jnp.einsum('blgh,gph->blgp', u, bb_i)
    shape = (1,) + bu_r.shape[1:]
    cum_r, cum_i, s_r, s_i = lax.associative_scan(
        complex_affine_combine,
        (jnp.broadcast_to(a_r, shape), jnp.broadcast_to(a_i, shape), bu_r, bu_i),
        reverse=reverse, axis=1)
    if h0 is not None:
        h_r, h_i = h0[0][:, None], h0[1][:, None]
        s_r, s_i = s_r + cum_r * h_r - cum_i * h_i, s_i + cum_r * h_i + cum_i * h_r
    return s_r, s_i


def s5_readout(s_r, s_i, c_re, c_im):
    return jnp.einsum('blgp,ghp->blgh', s_r, c_re) - jnp.einsum('blgp,ghp->blgh', s_i, c_im)


def s5_glu(y, w, bias):
    return y * jax.nn.sigmoid(jax.nn.gelu(y) @ w.astype(jnp.float32) + bias.astype(jnp.float32))


def s5_mixer(u_lat, u_ctx, p, ctx_out):
    b, l = u_lat.shape[:2]
    lc = u_ctx.shape[1]
    ul = u_lat.astype(jnp.float32).reshape(b, l, S5_G, S5_H)
    uc = u_ctx.astype(jnp.float32).reshape(b, lc, S5_G, S5_H)
    d_skip = p['s5_d'].astype(jnp.float32)
    y_lat = ul * d_skip
    y_ctx = uc * d_skip if ctx_out else None
    for direction in range(2):
        reverse = direction == 1
        disc = s5_discretize(p['lam_re'][direction], p['lam_im'][direction], p['log_dt'][direction],
                             p['b_re'][direction], p['b_im'][direction])
        c_re = p['c_re'][direction].astype(jnp.float32)
        c_im = p['c_im'][direction].astype(jnp.float32)
        sc_r, sc_i = s5_scan(uc, disc, None, reverse)
        edge = 0 if reverse else -1
        sl_r, sl_i = s5_scan(ul, disc, (sc_r[:, edge], sc_i[:, edge]), reverse)
        y_lat = y_lat + s5_readout(sl_r, sl_i, c_re, c_im)
        if ctx_out:
            y_ctx = y_ctx + s5_readout(sc_r, sc_i, c_re, c_im)
    out_lat = s5_glu(y_lat.reshape(b, l, S5_W), p['w_glu'], p['b_glu']).astype(u_lat.dtype)
    if not ctx_out:
        return out_lat, None
    out_ctx = s5_glu(y_ctx.reshape(b, lc, S5_W), p['w_glu'], p['b_glu']).astype(u_ctx.dtype)
    return out_lat, out_ctx


def gated_merge(h, branches, p):
    y = None
    for n, br in enumerate(branches):
        gate = jax.nn.sigmoid(h @ p['w_gate'][n] + p['b_gate'][n])
        term = gate * (br @ p['w_branch'][n])
        y = term if y is None else y + term
    return y @ p['w_o']


def token_mixer(h_lat, h_ctx, p, rope_mla, rope_gqa, ctx_out):
    z_lat = h_lat @ p['w_in']
    z_ctx = h_ctx @ p['w_in']
    cq_l, ckv_l, kr_l, u_l, gq_l, gk_l, gv_l = jnp.split(z_lat, IN_OFFSETS, axis=-1)
    cq_c, ckv_c, kr_c, u_c, gq_c, gk_c, gv_c = jnp.split(z_ctx, IN_OFFSETS, axis=-1)

    mk_c, mv_c = mla_kv(ckv_c, kr_c, p, None)
    mk_l, mv_l = mla_kv(ckv_l, kr_l, p, rope_mla)
    mla_l = blocked_attention(mla_q(cq_l, p, rope_mla), jnp.concatenate([mk_c, mk_l], axis=1),
                              jnp.concatenate([mv_c, mv_l], axis=1), MLA_SCALE)

    s5_l, s5_c = s5_mixer(u_l, u_c, p, ctx_out)

    gk_cc, gv_cc = gqa_kv(gk_c, gv_c, p, None)
    gk_ll, gv_ll = gqa_kv(gk_l, gv_l, p, rope_gqa)
    gqa_l = blocked_attention(gqa_q(gq_l, p, rope_gqa), jnp.concatenate([gk_cc, gk_ll], axis=1),
                              jnp.concatenate([gv_cc, gv_ll], axis=1), GQA_SCALE)

    out_lat = gated_merge(h_lat, (mla_l, s5_l, gqa_l), p)
    if not ctx_out:
        return out_lat, None
    mla_c = blocked_attention(mla_q(cq_c, p, None), mk_c, mv_c, MLA_SCALE)
    gqa_c = blocked_attention(gqa_q(gq_c, p, None), gk_cc, gv_cc, GQA_SCALE)
    out_ctx = gated_merge(h_ctx, (mla_c, s5_c, gqa_c), p)
    return out_lat, out_ctx


def hybrid_layer(x_lat, x_ctx, ml, mc, p, rope_mla, rope_gqa, last):
    g = p['norm_g']
    up, down = p['ffn_up'], p['ffn_down']
    x_lat = x_lat + 0.5 * ml[2] * swiglu(modulate(rms_norm(x_lat, g[0]), ml[0], ml[1]), up[0], down[0])
    x_ctx = x_ctx + 0.5 * mc[2] * swiglu(modulate(rms_norm(x_ctx, g[0]), mc[0], mc[1]), up[0], down[0])
    h_lat = modulate(rms_norm(x_lat, g[1]), ml[3], ml[4])
    h_ctx = modulate(rms_norm(x_ctx, g[1]), mc[3], mc[4])
    out_lat, out_ctx = token_mixer(h_lat, h_ctx, p, rope_mla, rope_gqa, not last)
    x_lat = x_lat + ml[5] * out_lat
    x_lat = x_lat + 0.5 * ml[8] * swiglu(modulate(rms_norm(x_lat, g[2]), ml[6], ml[7]), up[1], down[1])
    if last:
        return x_lat, None
    x_ctx = x_ctx + mc[5] * out_ctx
    x_ctx = x_ctx + 0.5 * mc[8] * swiglu(modulate(rms_norm(x_ctx, g[2]), mc[6], mc[7]), up[1], down[1])
    return x_lat, x_ctx


def setup_inputs(seed: int = 0) -> dict:
    key = jax.random.key(seed)
    ks = iter(jax.random.split(key, 40))
    f32 = jnp.float32
    D = D_MODEL

    def normal(shape, scale):
        return jax.random.normal(next(ks), shape, f32) * scale

    def gain(shape):
        return 1.0 + normal(shape, 0.02)

    return {
        'x': normal((BATCH, SEQ, D), 1.0),
        'c': normal((BATCH, D), 1.0),
        'ctx': normal((BATCH, CTX_LEN, D), 1.0),
        'c_ctx': normal((D,), 1.0),
        'w_mod': normal((DEPTH, D, N_MOD * D), 0.5 * D ** -0.5),
        'b_mod': normal((DEPTH, N_MOD * D), 0.02),
        'norm_g': gain((DEPTH, 3, D)),
        'w_ffn_up': normal((DEPTH, 2, D, 2 * D_FF), D ** -0.5),
        'w_ffn_down': normal((DEPTH, 2, D_FF, D), D_FF ** -0.5),
        'w_in': normal((DEPTH, D, D_IN), D ** -0.5),
        'mla_g_cq': gain((DEPTH, MLA_Q_LORA)),
        'mla_g_ckv': gain((DEPTH, MLA_KV_LORA)),
        'mla_w_uq': normal((DEPTH, MLA_Q_LORA, MLA_HEADS * (MLA_NOPE + MLA_ROPE)), MLA_Q_LORA ** -0.5),
        'mla_w_ukv': normal((DEPTH, MLA_KV_LORA, MLA_HEADS * (MLA_NOPE + MLA_V)), MLA_KV_LORA ** -0.5),
        'gqa_g_q': gain((DEPTH, GQA_HEAD_DIM)),
        'gqa_g_k': gain((DEPTH, GQA_HEAD_DIM)),
        's5_lam_re': -0.5 + normal((DEPTH, 2, S5_G, S5_P), 0.01),
        's5_lam_im': math.pi * jnp.arange(S5_P, dtype=f32) + normal((DEPTH, 2, S5_G, S5_P), 0.01),
        's5_log_dt': jax.random.uniform(next(ks), (DEPTH, 2, S5_G), f32,
                                        minval=math.log(S5_DT_MIN), maxval=math.log(S5_DT_MAX)),
        's5_b_re': normal((DEPTH, 2, S5_G, S5_P, S5_H), (2 * S5_H) ** -0.5),
        's5_b_im': normal((DEPTH, 2, S5_G, S5_P, S5_H), (2 * S5_H) ** -0.5),
        's5_c_re': normal((DEPTH, 2, S5_G, S5_H, S5_P), S5_P ** -0.5),
        's5_c_im': normal((DEPTH, 2, S5_G, S5_H, S5_P), S5_P ** -0.5),
        's5_d': normal((DEPTH, S5_G, S5_H), 1.0),
        's5_w_glu': normal((DEPTH, S5_W, S5_W), S5_W ** -0.5),
        's5_b_glu': normal((DEPTH, S5_W), 0.02),
        'w_gate': normal((DEPTH, N_BRANCH, D, D), D ** -0.5),
        'b_gate': normal((DEPTH, N_BRANCH, D), 0.02),
        'w_branch': normal((DEPTH, N_BRANCH, BRANCH_W, D), BRANCH_W ** -0.5),
        'w_o': normal((DEPTH, D, D), D ** -0.5),
        'final_g': gain((D,)),
    }


def reference(x, c, ctx, c_ctx, w_mod, b_mod, norm_g, w_ffn_up, w_ffn_down, w_in,
              mla_g_cq, mla_g_ckv, mla_w_uq, mla_w_ukv, gqa_g_q, gqa_g_k,
              s5_lam_re, s5_lam_im, s5_log_dt, s5_b_re, s5_b_im, s5_c_re, s5_c_im, s5_d,
              s5_w_glu, s5_b_glu, w_gate, b_gate, w_branch, w_o, final_g):
    b, l, d = x.shape
    rows, cols = grid_positions(l)
    rope_mla = axial_rope_tables(rows, cols, MLA_ROPE)
    rope_gqa = axial_rope_tables(rows, cols, GQA_HEAD_DIM)
    x_lat, x_ctx = x, ctx
    for li in range(DEPTH):
        m_lat = (jax.nn.silu(c) @ w_mod[li] + b_mod[li]).reshape(b, N_MOD, d)
        m_ctx = (jax.nn.silu(c_ctx) @ w_mod[li] + b_mod[li]).reshape(N_MOD, d)
        ml = [m_lat[:, i, None, :] for i in range(N_MOD)]
        mc = [m_ctx[i][None, None, :] for i in range(N_MOD)]
        p = {
            'norm_g': norm_g[li], 'ffn_up': w_ffn_up[li], 'ffn_down': w_ffn_down[li], 'w_in': w_in[li],
            'g_cq': mla_g_cq[li], 'g_ckv': mla_g_ckv[li], 'w_uq': mla_w_uq[li], 'w_ukv': mla_w_ukv[li],
            'g_q': gqa_g_q[li], 'g_k': gqa_g_k[li],
            'lam_re': s5_lam_re[li], 'lam_im': s5_lam_im[li], 'log_dt': s5_log_dt[li],
            'b_re': s5_b_re[li], 'b_im': s5_b_im[li], 'c_re': s5_c_re[li], 'c_im': s5_c_im[li],
            's5_d': s5_d[li], 'w_glu': s5_w_glu[li], 'b_glu': s5_b_glu[li],
            'w_gate': w_gate[li], 'b_gate': b_gate[li], 'w_branch': w_branch[li], 'w_o': w_o[li],
        }
        x_lat, x_ctx = hybrid_layer(x_lat, x_ctx, ml, mc, p, rope_mla, rope_gqa, li == DEPTH - 1)
    return rms_norm(x_lat, final_g)
```

```python
import functools
import math

import jax
import jax.numpy as jnp
from jax import lax
from jax.experimental import pallas as pl
from jax.experimental.pallas import tpu as pltpu

F32 = jnp.float32
BF16 = jnp.bfloat16

GRID_W = 64
ROPE_THETA = 10000.0
NORM_EPS = 1e-6
N_MOD = 9
MLA_HEADS = 8
MLA_NOPE = 128
MLA_ROPE = 64
MLA_V = 128
MLA_Q_LORA = 1536
MLA_KV_LORA = 512
MLA_SCALE = (MLA_NOPE + MLA_ROPE) ** -0.5
S5_W = 1024
S5_H = 16
S5_G = S5_W // S5_H
S5_P = 64
S5_LAMBDA_RE_MAX = -1e-4
GQA_Q_HEADS = 8
GQA_KV_HEADS = 2
GQA_HEAD_DIM = 128
GQA_SCALE = GQA_HEAD_DIM ** -0.5

LANES = 128
SUBLANES = 8
VMEM_LIMIT_BYTES = 56 * 1024 * 1024

ZC_CQ = 0
ZC_CKV = 1536
ZC_U = 2048
ZC_GQ = 3072
ZC_GK = 4096
ZC_GV = 4352
ZC_KR = 4608
ZC_KR_ROT = 4736
ZC_GK_ROT = 4864
ZC_GQ_ROT = 5120
Z_WIDTH = 6144

S5_BLK_G = 16
S5_NBLK = S5_G // S5_BLK_G
S5_BLK_IN = S5_BLK_G * S5_H
S5_BLK_ST = S5_BLK_G * S5_P
S5_STATE = S5_G * S5_P
S5_LANE_GRP = 512
S5_TIME_STEPS = 64
S5_SCAN_UNROLL = 2

ATTN_TQ = 256
MLA_HEADS_PER_STEP = 4


def _cparams(*sem):
    return pltpu.CompilerParams(dimension_semantics=sem, vmem_limit_bytes=VMEM_LIMIT_BYTES)


def _pick_tile(n, cands=(512, 256, 128, 64, 32, 16, 8)):
    for c in cands:
        if n % c == 0:
            return c
    raise ValueError(f"no tile for {n}")


def _dot(a, b):
    return jnp.dot(a, b, preferred_element_type=F32)


def _dot_nt(a, b):
    return lax.dot_general(a, b, (((1,), (1,)), ((), ())), preferred_element_type=F32)


def _rms(x, g):
    return x * lax.rsqrt(jnp.mean(x * x, axis=-1, keepdims=True) + NORM_EPS) * g


def _normmod(x, g, shift, scale):
    return _rms(x, g) * (1 + scale) + shift


def _mod_body(c_ref, w_ref, b_ref, o_ref):
    a = jax.nn.silu(c_ref[...]).astype(BF16)
    o_ref[0] = _dot(a, w_ref[0].astype(BF16)) + b_ref[0]


def _mod_vectors(c_all, w_mod, b_mod):
    depth, d, n = w_mod.shape
    rows = c_all.shape[0]
    tn = 1024
    return pl.pallas_call(
        _mod_body,
        grid=(depth, n // tn),
        in_specs=[
            pl.BlockSpec((rows, d), lambda l, j: (0, 0)),
            pl.BlockSpec((1, d, tn), lambda l, j: (l, 0, j)),
            pl.BlockSpec((1, 1, tn), lambda l, j: (l, 0, j)),
        ],
        out_specs=pl.BlockSpec((1, rows, tn), lambda l, j: (l, 0, j)),
        out_shape=jax.ShapeDtypeStruct((depth, rows, n), F32),
        compiler_params=_cparams("arbitrary", "arbitrary"),
        name="adaln_mod",
    )(c_all, w_mod, b_mod.reshape(depth, 1, n))


def _ffn_body(x_ref, mod_ref, g_ref, wg_ref, wu_ref, wd_ref, gout_ref, o_ref, h_ref, *, k0, nj, norm_out):
    j = pl.program_id(1)

    @pl.when(j == 0)
    def _():
        h = _normmod(x_ref[...], g_ref[...], mod_ref[0, k0:k0 + 1, :], mod_ref[0, k0 + 1:k0 + 2, :])
        h_ref[...] = h.astype(BF16)
        o_ref[...] = jnp.zeros_like(o_ref)

    h = h_ref[...]
    act = (jax.nn.silu(_dot(h, wg_ref[...])) * _dot(h, wu_ref[...])).astype(BF16)
    o_ref[...] += _dot(act, wd_ref[...])

    @pl.when(j == nj - 1)
    def _():
        y = x_ref[...] + 0.5 * mod_ref[0, k0 + 2:k0 + 3, :] * o_ref[...]
        o_ref[...] = _rms(y, gout_ref[...]) if norm_out else y


def _ffn(x, mod, rows, g, w_up, w_down, k0, g_out=None):
    t, d = x.shape
    dff = w_down.shape[0]
    row_fn, seq = rows
    tm = _pick_tile(seq)
    tf = 512
    nj = dff // tf
    return pl.pallas_call(
        functools.partial(_ffn_body, k0=k0, nj=nj, norm_out=g_out is not None),
        grid=(t // tm, nj),
        in_specs=[
            pl.BlockSpec((tm, d), lambda i, j: (i, 0)),
            pl.BlockSpec((1, N_MOD, d), lambda i, j: (row_fn(i, tm), 0, 0)),
            pl.BlockSpec((1, d), lambda i, j: (0, 0)),
            pl.BlockSpec((d, tf), lambda i, j: (0, j)),
            pl.BlockSpec((d, tf), lambda i, j: (0, j + nj)),
            pl.BlockSpec((tf, d), lambda i, j: (j, 0)),
            pl.BlockSpec((1, d), lambda i, j: (0, 0)),
        ],
        out_specs=pl.BlockSpec((tm, d), lambda i, j: (i, 0)),
        out_shape=jax.ShapeDtypeStruct((t, d), F32),
        scratch_shapes=[pltpu.VMEM((tm, d), BF16)],
        compiler_params=_cparams("parallel", "arbitrary"),
        name="ffn_swiglu",
    )(x, mod, g.reshape(1, d), w_up, w_up, w_down, (g if g_out is None else g_out).reshape(1, d))


def _mixin_body(x_ref, mod_ref, g_ref, w_ref, h_ref, z_ref, *, k0):
    @pl.when(pl.program_id(1) == 0)
    def _():
        h = _normmod(x_ref[...], g_ref[...], mod_ref[0, k0:k0 + 1, :], mod_ref[0, k0 + 1:k0 + 2, :])
        h_ref[...] = h.astype(BF16)

    z_ref[...] = _dot(h_ref[...], w_ref[...])


def _mixer_in(x, mod, rows, g, w_p, k0):
    t, d = x.shape
    n = w_p.shape[1]
    row_fn, seq = rows
    tm = _pick_tile(seq, (1024, 512, 256, 128, 64, 32, 16, 8))
    tn = 1024
    return pl.pallas_call(
        functools.partial(_mixin_body, k0=k0),
        grid=(t // tm, n // tn),
        in_specs=[
            pl.BlockSpec((tm, d), lambda i, j: (i, 0)),
            pl.BlockSpec((1, N_MOD, d), lambda i, j: (row_fn(i, tm), 0, 0)),
            pl.BlockSpec((1, d), lambda i, j: (0, 0)),
            pl.BlockSpec((d, tn), lambda i, j: (0, j)),
        ],
        out_specs=[
            pl.BlockSpec((tm, d), lambda i, j: (i, 0)),
            pl.BlockSpec((tm, tn), lambda i, j: (i, j)),
        ],
        out_shape=[jax.ShapeDtypeStruct((t, d), BF16), jax.ShapeDtypeStruct((t, n), F32)],
        compiler_params=_cparams("parallel", "arbitrary"),
        name="mixer_in",
    )(x, mod, g.reshape(1, d), w_p)


def _mla_proj_body(cq_ref, ckv_ref, kr_ref, krr_ref, cos_ref, sin_ref, gq_ref, gkv_ref,
                   wq_ref, wqr_ref, wkv_ref, q_ref, kv_ref, kro_ref):
    cos = cos_ref[...]
    sin = sin_ref[...]
    nq = _rms(cq_ref[...], gq_ref[...]).astype(BF16)
    qm = _dot(nq, wq_ref[...])
    qr = _dot(nq, wqr_ref[...])
    hw = MLA_NOPE + LANES
    for h in range(MLA_HEADS):
        q_ref[:, h * hw:h * hw + MLA_NOPE] = qm[:, h * hw:h * hw + MLA_NOPE].astype(BF16)
        rope = qm[:, h * hw + MLA_NOPE:(h + 1) * hw] * cos + qr[:, h * LANES:(h + 1) * LANES] * sin
        q_ref[:, h * hw + MLA_NOPE:(h + 1) * hw] = rope.astype(BF16)
    nkv = _rms(ckv_ref[...], gkv_ref[...]).astype(BF16)
    kv_ref[...] = _dot(nkv, wkv_ref[...]).astype(BF16)
    kro_ref[...] = (kr_ref[...] * cos + krr_ref[...] * sin).astype(BF16)


def _mla_proj(z, cos, sin, g_cq, g_ckv, wq, wqr, wkv, tm):
    t = z.shape[0]
    nblk = cos.shape[0] // tm
    hw = MLA_NOPE + LANES
    return pl.pallas_call(
        _mla_proj_body,
        grid=(t // tm,),
        in_specs=[
            pl.BlockSpec((tm, MLA_Q_LORA), lambda i: (i, ZC_CQ // MLA_Q_LORA)),
            pl.BlockSpec((tm, MLA_KV_LORA), lambda i: (i, ZC_CKV // MLA_KV_LORA)),
            pl.BlockSpec((tm, LANES), lambda i: (i, ZC_KR // LANES)),
            pl.BlockSpec((tm, LANES), lambda i: (i, ZC_KR_ROT // LANES)),
            pl.BlockSpec((tm, LANES), lambda i: (i % nblk, 0)),
            pl.BlockSpec((tm, LANES), lambda i: (i % nblk, 0)),
            pl.BlockSpec((1, MLA_Q_LORA), lambda i: (0, 0)),
            pl.BlockSpec((1, MLA_KV_LORA), lambda i: (0, 0)),
            pl.BlockSpec(wq.shape, lambda i: (0, 0)),
            pl.BlockSpec(wqr.shape, lambda i: (0, 0)),
            pl.BlockSpec(wkv.shape, lambda i: (0, 0)),
        ],
        out_specs=[
            pl.BlockSpec((tm, MLA_HEADS * hw), lambda i: (i, 0)),
            pl.BlockSpec((tm, wkv.shape[1]), lambda i: (i, 0)),
            pl.BlockSpec((tm, LANES), lambda i: (i, 0)),
        ],
        out_shape=[
            jax.ShapeDtypeStruct((t, MLA_HEADS * hw), BF16),
            jax.ShapeDtypeStruct((t, wkv.shape[1]), BF16),
            jax.ShapeDtypeStruct((t, LANES), BF16),
        ],
        compiler_params=_cparams("parallel"),
        name="mla_proj",
    )(z, z, z, z, cos, sin, g_cq.reshape(1, -1), g_ckv.reshape(1, -1), wq, wqr, wkv)


def _gqa_prep_body(gq_ref, gqr_ref, gk_ref, gkr_ref, gv_ref, cos_ref, sin_ref,
                   g_ref, q_ref, k_ref, v_ref):
    cos = cos_ref[...]
    sin = sin_ref[...]
    hd = GQA_HEAD_DIM

    def head(x, xr, g, gr):
        r = lax.rsqrt(jnp.mean(x * x, axis=-1, keepdims=True) + NORM_EPS)
        return ((x * r * g) * cos + (xr * r * gr) * sin).astype(BF16)

    for h in range(GQA_Q_HEADS):
        q_ref[:, h * hd:(h + 1) * hd] = head(gq_ref[:, h * hd:(h + 1) * hd], gqr_ref[:, h * hd:(h + 1) * hd],
                                             g_ref[0:1, :], g_ref[1:2, :])
    for h in range(GQA_KV_HEADS):
        k_ref[:, h * hd:(h + 1) * hd] = head(gk_ref[:, h * hd:(h + 1) * hd], gkr_ref[:, h * hd:(h + 1) * hd],
                                             g_ref[2:3, :], g_ref[3:4, :])
    v_ref[...] = gv_ref[...].astype(BF16)


def _gqa_prep(z, cos, sin, gains, tm):
    t = z.shape[0]
    nblk = cos.shape[0] // tm
    qw = GQA_Q_HEADS * GQA_HEAD_DIM
    kw = GQA_KV_HEADS * GQA_HEAD_DIM
    return pl.pallas_call(
        _gqa_prep_body,
        grid=(t // tm,),
        in_specs=[
            pl.BlockSpec((tm, qw), lambda i: (i, ZC_GQ // qw)),
            pl.BlockSpec((tm, qw), lambda i: (i, ZC_GQ_ROT // qw)),
            pl.BlockSpec((tm, kw), lambda i: (i, ZC_GK // kw)),
            pl.BlockSpec((tm, kw), lambda i: (i, ZC_GK_ROT // kw)),
            pl.BlockSpec((tm, kw), lambda i: (i, ZC_GV // kw)),
            pl.BlockSpec((tm, LANES), lambda i: (i % nblk, 0)),
            pl.BlockSpec((tm, LANES), lambda i: (i % nblk, 0)),
            pl.BlockSpec((4, GQA_HEAD_DIM), lambda i: (0, 0)),
        ],
        out_specs=[
            pl.BlockSpec((tm, qw), lambda i: (i, 0)),
            pl.BlockSpec((tm, kw), lambda i: (i, 0)),
            pl.BlockSpec((tm, kw), lambda i: (i, 0)),
        ],
        out_shape=[
            jax.ShapeDtypeStruct((t, qw), BF16),
            jax.ShapeDtypeStruct((t, kw), BF16),
            jax.ShapeDtypeStruct((t, kw), BF16),
        ],
        compiler_params=_cparams("parallel"),
        name="gqa_prep",
    )(z, z, z, z, z, cos, sin, gains)


def _attn_body(*refs, nseg, mla, coef, heads, seg_lens):
    q_ref = refs[0]
    per = 3 if mla else 2
    o_ref, k_scr, v_scr = refs[1 + nseg * per:]
    dq = q_ref.shape[1] // heads
    dv = o_ref.shape[1] // heads

    @pl.when(pl.program_id(2) == 0)
    def _():
        off = 0
        for s in range(nseg):
            seg = refs[1 + s * per:1 + (s + 1) * per]
            rows = slice(off, off + seg_lens[s])
            if mla:
                for g in range(heads):
                    k_scr[g, rows, :MLA_NOPE] = seg[0][:, g * MLA_NOPE:(g + 1) * MLA_NOPE]
                    k_scr[g, rows, MLA_NOPE:] = seg[1][...]
                    v_scr[g, rows, :] = seg[2][:, g * MLA_V:(g + 1) * MLA_V]
            else:
                k_scr[0, rows, :] = seg[0][...]
                v_scr[0, rows, :] = seg[1][...]
            off += seg_lens[s]

    for g in range(heads):
        kg = g if mla else 0
        t = _dot_nt(q_ref[:, g * dq:(g + 1) * dq], k_scr[kg]) * coef
        p = jnp.exp2(t - t.max(axis=-1, keepdims=True))
        l = p.sum(axis=-1, keepdims=True)
        o = _dot(p.astype(BF16), v_scr[kg])
        o_ref[:, g * dv:(g + 1) * dv] = (o / l).astype(BF16)


def _attention(q, segs, batch, mla):
    tq_total = q.shape[0]
    lq = tq_total // batch
    tq = _pick_tile(lq, (ATTN_TQ, 128, 64, 32, 16, 8))
    nq = lq // tq
    if mla:
        heads, dq, dv, nsteps = MLA_HEADS_PER_STEP, MLA_NOPE + LANES, MLA_V, MLA_HEADS // MLA_HEADS_PER_STEP
        scale = MLA_SCALE
    else:
        heads, dq, dv, nsteps = GQA_Q_HEADS // GQA_KV_HEADS, GQA_HEAD_DIM, GQA_HEAD_DIM, GQA_KV_HEADS
        scale = GQA_SCALE
    in_specs = [pl.BlockSpec((tq, heads * dq), lambda b, h, i: (b * nq + i, h))]
    args = [q]
    seg_lens = []
    for seg in segs:
        ls = seg[0].shape[0] // batch
        seg_lens.append(ls)
        if mla:
            kv, kr = seg
            v_blk0 = MLA_HEADS * MLA_NOPE // (heads * MLA_V)
            in_specs += [
                pl.BlockSpec((ls, heads * MLA_NOPE), lambda b, h, i: (b, h)),
                pl.BlockSpec((ls, LANES), lambda b, h, i: (b, 0)),
                pl.BlockSpec((ls, heads * MLA_V), lambda b, h, i: (b, v_blk0 + h)),
            ]
            args += [kv, kr, kv]
        else:
            k, v = seg
            in_specs += [
                pl.BlockSpec((ls, GQA_HEAD_DIM), lambda b, h, i: (b, h)),
                pl.BlockSpec((ls, GQA_HEAD_DIM), lambda b, h, i: (b, h)),
            ]
            args += [k, v]
    ltot = sum(seg_lens)
    kheads = heads if mla else 1
    return pl.pallas_call(
        functools.partial(_attn_body, nseg=len(segs), mla=mla, coef=scale * math.log2(math.e), heads=heads,
                          seg_lens=tuple(seg_lens)),
        grid=(batch, nsteps, nq),
        in_specs=in_specs,
        out_specs=pl.BlockSpec((tq, heads * dv), lambda b, h, i: (b * nq + i, h)),
        out_shape=jax.ShapeDtypeStruct((tq_total, nsteps * heads * dv), BF16),
        scratch_shapes=[pltpu.VMEM((kheads, ltot, dq), BF16), pltpu.VMEM((kheads, ltot, dv), BF16)],
        compiler_params=_cparams("parallel", "parallel", "arbitrary"),
        name="attn_mla" if mla else "attn_gqa",
    )(*args)


def _s5_scan_body(u_ref, bre_ref, bim_ref, cre_ref, cim_ref, ar_ref, ai_ref, y_ref, sre, sim, car_r, car_i,
                  *, reverse, ts):
    @pl.when(pl.program_id(0) == 0)
    def _():
        car_r[...] = jnp.zeros_like(car_r)
        car_i[...] = jnp.zeros_like(car_i)

    ub = u_ref[...].astype(BF16)
    for k in range(S5_NBLK):
        uk = ub[:, k * S5_BLK_IN:(k + 1) * S5_BLK_IN]
        sre[:, k * S5_BLK_ST:(k + 1) * S5_BLK_ST] = _dot(uk, bre_ref[k])
        sim[:, k * S5_BLK_ST:(k + 1) * S5_BLK_ST] = _dot(uk, bim_ref[k])

    for lg in range(S5_STATE // S5_LANE_GRP):
        cols = slice(lg * S5_LANE_GRP, (lg + 1) * S5_LANE_GRP)
        ar = ar_ref[:, cols]
        ai = ai_ref[:, cols]

        def step(t, carry, cols=cols, ar=ar, ai=ai):
            sr, si = carry
            r0 = pl.multiple_of(((ts - 1 - t) if reverse else t) * SUBLANES, SUBLANES)
            nr = ar * sr - ai * si + sre[pl.ds(r0, SUBLANES), cols]
            ni = ar * si + ai * sr + sim[pl.ds(r0, SUBLANES), cols]
            sre[pl.ds(r0, SUBLANES), cols] = nr
            sim[pl.ds(r0, SUBLANES), cols] = ni
            return nr, ni

        sr, si = lax.fori_loop(0, ts, step, (car_r[:, cols], car_i[:, cols]), unroll=S5_SCAN_UNROLL)
        car_r[:, cols] = sr
        car_i[:, cols] = si

    for k in range(S5_NBLK):
        st = slice(k * S5_BLK_ST, (k + 1) * S5_BLK_ST)
        y_ref[:, k * S5_BLK_IN:(k + 1) * S5_BLK_IN] = (
            _dot(sre[:, st].astype(BF16), cre_ref[k]) - _dot(sim[:, st].astype(BF16), cim_ref[k]))


def _s5_scan(u_tb, prm, reverse, n_ctx_steps, ts):
    rows = u_tb.shape[0]
    rt = ts * SUBLANES
    nchunk = rows // rt
    nctx = n_ctx_steps // ts
    if reverse:
        blk = lambda c: jnp.where(c < nctx, nctx - 1 - c, nchunk - 1 - (c - nctx))
    else:
        blk = lambda c: c
    full = lambda a: pl.BlockSpec(a.shape, lambda c: (0,) * a.ndim)
    bre, bim, cre, cim, ar, ai = prm
    return pl.pallas_call(
        functools.partial(_s5_scan_body, reverse=reverse, ts=ts),
        grid=(nchunk,),
        in_specs=[pl.BlockSpec((rt, S5_W), lambda c: (blk(c), 0)),
                  full(bre), full(bim), full(cre), full(cim), full(ar), full(ai)],
        out_specs=pl.BlockSpec((rt, S5_W), lambda c: (blk(c), 0)),
        out_shape=jax.ShapeDtypeStruct((rows, S5_W), F32),
        scratch_shapes=[
            pltpu.VMEM((rt, S5_STATE), F32), pltpu.VMEM((rt, S5_STATE), F32),
            pltpu.VMEM((SUBLANES, S5_STATE), F32), pltpu.VMEM((SUBLANES, S5_STATE), F32),
        ],
        compiler_params=_cparams("arbitrary"),
        name="s5_scan_bwd" if reverse else "s5_scan_fwd",
    )(u_tb, bre, bim, cre, cim, ar, ai)


def _s5_glu_body(u_ref, yf_ref, yb_ref, d_ref, w_ref, b_ref, o_ref):
    y = u_ref[...] * d_ref[...] + yf_ref[...] + yb_ref[...]
    gate = jax.nn.sigmoid(_dot(jax.nn.gelu(y).astype(BF16), w_ref[...]) + b_ref[...])
    o_ref[...] = (y * gate).astype(BF16)


def _s5_glu(u, yf, yb, d_skip, w_glu, b_glu):
    t = u.shape[0]
    tm = _pick_tile(t)
    row = lambda i: (i, 0)
    const = lambda i: (0, 0)
    return pl.pallas_call(
        _s5_glu_body,
        grid=(t // tm,),
        in_specs=[
            pl.BlockSpec((tm, S5_W), row), pl.BlockSpec((tm, S5_W), row), pl.BlockSpec((tm, S5_W), row),
            pl.BlockSpec((1, S5_W), const), pl.BlockSpec((S5_W, S5_W), const), pl.BlockSpec((1, S5_W), const),
        ],
        out_specs=pl.BlockSpec((tm, S5_W), row),
        out_shape=jax.ShapeDtypeStruct((t, S5_W), BF16),
        compiler_params=_cparams("parallel"),
        name="s5_glu",
    )(u, yf, yb, d_skip, w_glu, b_glu)


def _merge_body(h_ref, b0_ref, b1_ref, b2_ref, x_ref, mod_ref, wg_ref, bg_ref, wb_ref, wo_ref, o_ref, *, k0, nj):
    j = pl.program_id(1)

    @pl.when(j == 0)
    def _():
        o_ref[...] = jnp.zeros_like(o_ref)

    h = h_ref[...]
    y = None
    for n, br in enumerate((b0_ref, b1_ref, b2_ref)):
        gate = jax.nn.sigmoid(_dot(h, wg_ref[n]) + bg_ref[n])
        term = gate * _dot(br[...], wb_ref[n])
        y = term if y is None else y + term
    o_ref[...] += _dot(y.astype(BF16), wo_ref[...])

    @pl.when(j == nj - 1)
    def _():
        o_ref[...] = x_ref[...] + mod_ref[0, k0:k0 + 1, :] * o_ref[...]


def _merge(h, branches, x, mod, rows, w_gate, b_gate, w_branch, w_o, k0):
    t, d = x.shape
    bw = w_branch.shape[1]
    row_fn, seq = rows
    tm = _pick_tile(seq)
    tj = 256
    nj = d // tj
    return pl.pallas_call(
        functools.partial(_merge_body, k0=k0, nj=nj),
        grid=(t // tm, nj),
        in_specs=[
            pl.BlockSpec((tm, d), lambda i, j: (i, 0)),
            pl.BlockSpec((tm, bw), lambda i, j: (i, 0)),
            pl.BlockSpec((tm, bw), lambda i, j: (i, 0)),
            pl.BlockSpec((tm, bw), lambda i, j: (i, 0)),
            pl.BlockSpec((tm, d), lambda i, j: (i, 0)),
            pl.BlockSpec((1, N_MOD, d), lambda i, j: (row_fn(i, tm), 0, 0)),
            pl.BlockSpec((3, d, tj), lambda i, j: (0, 0, j)),
            pl.BlockSpec((3, 1, tj), lambda i, j: (0, 0, j)),
            pl.BlockSpec((3, bw, tj), lambda i, j: (0, 0, j)),
            pl.BlockSpec((tj, d), lambda i, j: (j, 0)),
        ],
        out_specs=pl.BlockSpec((tm, d), lambda i, j: (i, 0)),
        out_shape=jax.ShapeDtypeStruct((t, d), F32),
        compiler_params=_cparams("parallel", "arbitrary"),
        name="gated_merge",
    )(h, *branches, x, mod, w_gate, b_gate, w_branch, w_o)


def _rot_half(v):
    v1, v2 = jnp.split(v, 2, axis=-1)
    return jnp.concatenate([-v2, v1], axis=-1)


def _rot_axial(v):
    a, b = jnp.split(v, 2, axis=-1)
    return jnp.concatenate([_rot_half(a), _rot_half(b)], axis=-1)


def _swap_half(v):
    v1, v2 = jnp.split(v, 2, axis=-1)
    return jnp.concatenate([v2, v1], axis=-1)


def _swap_axial(v):
    a, b = jnp.split(v, 2, axis=-1)
    return jnp.concatenate([_swap_half(a), _swap_half(b)], axis=-1)


def _rope_tables(n_tokens, d_rot):
    rows = n_tokens // GRID_W
    r, col = jnp.meshgrid(jnp.arange(rows, dtype=jnp.int32), jnp.arange(GRID_W, dtype=jnp.int32), indexing='ij')
    half = d_rot // 2
    freqs = ROPE_THETA ** (-jnp.arange(0, half, 2, dtype=F32) / half)

    def table(pos):
        ang = pos.reshape(-1).astype(F32)[:, None] * freqs[None, :]
        ang = jnp.concatenate([ang, ang], axis=-1)
        return jnp.cos(ang), jnp.sin(ang)

    cr, sr = table(r)
    cc, sc = table(col)
    return jnp.concatenate([cr, cc], axis=-1), jnp.concatenate([sr, sc], axis=-1)


def _pad_lanes(a, width):
    return jnp.pad(a, [(0, 0)] * (a.ndim - 1) + [(0, width - a.shape[-1])])


def _prep_w_in(w):
    d = w.shape[0]
    offs = [0, MLA_Q_LORA, MLA_KV_LORA, MLA_ROPE, S5_W, GQA_Q_HEADS * GQA_HEAD_DIM,
            GQA_KV_HEADS * GQA_HEAD_DIM, GQA_KV_HEADS * GQA_HEAD_DIM]
    offs = [sum(offs[:i + 1]) for i in range(len(offs))]
    cq, ckv, kr, u, gq, gk, gv = [w[:, offs[i]:offs[i + 1]] for i in range(7)]
    rot_heads = lambda a: _rot_axial(a.reshape(d, -1, GQA_HEAD_DIM)).reshape(d, -1)
    cols = [cq, ckv, u, gq, gk, gv, _pad_lanes(kr, LANES), _pad_lanes(_rot_axial(kr), LANES),
            rot_heads(gk), rot_heads(gq)]
    wp = jnp.concatenate(cols, axis=1)
    assert wp.shape[1] == Z_WIDTH
    return wp.astype(BF16)


def _prep_mla(w_uq, w_ukv):
    r = w_uq.shape[0]
    wq = w_uq.reshape(r, MLA_HEADS, MLA_NOPE + MLA_ROPE)
    nope, rope = wq[..., :MLA_NOPE], wq[..., MLA_NOPE:]
    wq_main = jnp.concatenate([nope, _pad_lanes(rope, LANES)], axis=-1).reshape(r, -1)
    wq_rot = _pad_lanes(_rot_axial(rope), LANES).reshape(r, -1)
    rk = w_ukv.shape[0]
    wkv = w_ukv.reshape(rk, MLA_HEADS, MLA_NOPE + MLA_V)
    wkv = jnp.concatenate([wkv[..., :MLA_NOPE].reshape(rk, -1), wkv[..., MLA_NOPE:].reshape(rk, -1)], axis=1)
    return wq_main.astype(BF16), wq_rot.astype(BF16), wkv.astype(BF16)


def _s5_discretize(lam_re, lam_im, log_dt, b_re, b_im):
    lr = jnp.minimum(lam_re.astype(F32), S5_LAMBDA_RE_MAX)
    li = lam_im.astype(F32)
    dt = jnp.exp(log_dt.astype(F32))[:, None]
    mag = jnp.exp(lr * dt)
    a_r, a_i = mag * jnp.cos(li * dt), mag * jnp.sin(li * dt)
    den = lr * lr + li * li
    n_r, n_i = a_r - 1.0, a_i
    f_r = (n_r * lr + n_i * li) / den
    f_i = (n_i * lr - n_r * li) / den
    br, bi = b_re.astype(F32), b_im.astype(F32)
    bb_r = f_r[..., None] * br - f_i[..., None] * bi
    bb_i = f_r[..., None] * bi + f_i[..., None] * br
    return a_r, a_i, bb_r, bb_i


def _block_diag(w):
    g, a, b = w.shape
    w4 = w.reshape(S5_NBLK, S5_BLK_G, a, b)
    eye = jnp.eye(S5_BLK_G, dtype=w.dtype)
    out = jnp.einsum('kgab,gh->kgahb', w4, eye)
    return out.reshape(S5_NBLK, S5_BLK_G * a, S5_BLK_G * b)


def _prep_s5(lam_re, lam_im, log_dt, b_re, b_im, c_re, c_im):
    a_r, a_i, bb_r, bb_i = _s5_discretize(lam_re, lam_im, log_dt, b_re, b_im)
    bre = _block_diag(jnp.swapaxes(bb_r, 1, 2)).astype(BF16)
    bim = _block_diag(jnp.swapaxes(bb_i, 1, 2)).astype(BF16)
    cre = _block_diag(jnp.swapaxes(c_re.astype(F32), 1, 2)).astype(BF16)
    cim = _block_diag(jnp.swapaxes(c_im.astype(F32), 1, 2)).astype(BF16)
    ar = jnp.broadcast_to(a_r.reshape(1, -1), (SUBLANES, S5_STATE))
    ai = jnp.broadcast_to(a_i.reshape(1, -1), (SUBLANES, S5_STATE))
    return bre, bim, cre, cim, ar, ai


def kernel(x, c, ctx, c_ctx, w_mod, b_mod, norm_g, w_ffn_up, w_ffn_down, w_in, mla_g_cq, mla_g_ckv, mla_w_uq, mla_w_ukv, gqa_g_q, gqa_g_k, s5_lam_re, s5_lam_im, s5_log_dt, s5_b_re, s5_b_im, s5_c_re, s5_c_im, s5_d, s5_w_glu, s5_b_glu, w_gate, b_gate, w_branch, w_o, final_g):
    b, l, d = x.shape
    lc = ctx.shape[1]
    depth = w_mod.shape[0]

    n_rows = -(-(b + 1) // SUBLANES) * SUBLANES
    c_all = jnp.concatenate([c, c_ctx[None, :], jnp.zeros((n_rows - b - 1, d), F32)], axis=0)
    mod_all = _mod_vectors(c_all, w_mod, b_mod).reshape(depth, n_rows, N_MOD, d)
    lat_row = (lambda i, tm: (i * tm) // l, l)
    ctx_row = (lambda i, tm: b, b * lc)

    assert b == SUBLANES, "the S5 scan lays the batch on the 8 sublanes"
    tp = _pick_tile(math.gcd(l, lc), (256, 128, 64, 32, 16, 8))
    ts5 = _pick_tile(math.gcd(l, lc), (S5_TIME_STEPS, 32, 16, 8))
    cos_m, sin_m = _rope_tables(l, MLA_ROPE)
    cos_m, sin_m = _pad_lanes(cos_m, LANES), _pad_lanes(sin_m, LANES)
    cos_g, sin_g = _rope_tables(l, GQA_HEAD_DIM)
    ones_t, zeros_t = jnp.ones((tp, LANES), F32), jnp.zeros((tp, LANES), F32)

    x_lat = x.reshape(b * l, d)
    x_ctx = ctx.reshape(b * lc, d)
    for li in range(depth):
        last = li == depth - 1
        mod = mod_all[li]
        g = norm_g[li]
        up = [w_ffn_up[li, f].astype(BF16) for f in range(2)]
        down = [w_ffn_down[li, f].astype(BF16) for f in range(2)]

        x_lat = _ffn(x_lat, mod, lat_row, g[0], up[0], down[0], 0)
        x_ctx = _ffn(x_ctx, mod, ctx_row, g[0], up[0], down[0], 0)

        w_p = _prep_w_in(w_in[li])
        h_lat, z_lat = _mixer_in(x_lat, mod, lat_row, g[1], w_p, 3)
        h_ctx, z_ctx = _mixer_in(x_ctx, mod, ctx_row, g[1], w_p, 3)

        wq, wqr, wkv = _prep_mla(mla_w_uq[li], mla_w_ukv[li])
        q_l, kv_l, kr_l = _mla_proj(z_lat, cos_m, sin_m, mla_g_cq[li], mla_g_ckv[li], wq, wqr, wkv, tp)
        q_c, kv_c, kr_c = _mla_proj(z_ctx, ones_t, zeros_t, mla_g_cq[li], mla_g_ckv[li], wq, wqr, wkv, tp)
        gains = jnp.stack([gqa_g_q[li], _swap_axial(gqa_g_q[li]), gqa_g_k[li], _swap_axial(gqa_g_k[li])]).astype(F32)
        gq_l, gk_l, gv_l = _gqa_prep(z_lat, cos_g, sin_g, gains, tp)
        gq_c, gk_c, gv_c = _gqa_prep(z_ctx, ones_t, zeros_t, gains, tp)

        mla_l = _attention(q_l, [(kv_c, kr_c), (kv_l, kr_l)], b, True)
        gqa_l = _attention(gq_l, [(gk_c, gv_c), (gk_l, gv_l)], b, False)

        u_tb = jnp.concatenate([
            jnp.swapaxes(z_ctx[:, ZC_U:ZC_U + S5_W].reshape(b, lc, S5_W), 0, 1),
            jnp.swapaxes(z_lat[:, ZC_U:ZC_U + S5_W].reshape(b, l, S5_W), 0, 1)], axis=0).reshape((lc + l) * b, S5_W)
        ys = []
        for direction in range(2):
            prm = _prep_s5(s5_lam_re[li, direction], s5_lam_im[li, direction], s5_log_dt[li, direction],
                           s5_b_re[li, direction], s5_b_im[li, direction],
                           s5_c_re[li, direction], s5_c_im[li, direction])
            ys.append(_s5_scan(u_tb, prm, direction == 1, lc, ts5))
        d_skip = s5_d[li].reshape(1, S5_W).astype(F32)
        w_glu = s5_w_glu[li].astype(BF16)
        b_glu = s5_b_glu[li].reshape(1, S5_W).astype(F32)
        s5_tb = _s5_glu(u_tb, ys[0], ys[1], d_skip, w_glu, b_glu).reshape(lc + l, b, S5_W)
        s5_l = jnp.swapaxes(s5_tb[lc:], 0, 1).reshape(b * l, S5_W)

        wg = w_gate[li].astype(BF16)
        bg = b_gate[li].reshape(3, 1, d).astype(F32)
        wb = w_branch[li].astype(BF16)
        wo = w_o[li].astype(BF16)
        x_lat_new = _merge(h_lat, (mla_l, s5_l, gqa_l), x_lat, mod, lat_row, wg, bg, wb, wo, 5)
        if not last:
            mla_c = _attention(q_c, [(kv_c, kr_c)], b, True)
            gqa_c = _attention(gq_c, [(gk_c, gv_c)], b, False)
            s5_c = jnp.swapaxes(s5_tb[:lc], 0, 1).reshape(b * lc, S5_W)
            x_ctx = _merge(h_ctx, (mla_c, s5_c, gqa_c), x_ctx, mod, ctx_row, wg, bg, wb, wo, 5)
            x_ctx = _ffn(x_ctx, mod, ctx_row, g[2], up[1], down[1], 6)
        x_lat = _ffn(x_lat_new, mod, lat_row, g[2], up[1], down[1], 6, final_g if last else None)

    return x_lat.reshape(b, l, d)
```

```python
import functools
import math

import jax
import jax.numpy as jnp
from jax import lax
from jax.experimental import pallas as pl
from jax.experimental.pallas import tpu as pltpu

F32 = jnp.float32
BF16 = jnp.bfloat16

GRID_W = 64
ROPE_THETA = 10000.0
NORM_EPS = 1e-6
N_MOD = 9
MLA_HEADS = 8
MLA_NOPE = 128
MLA_ROPE = 64
MLA_V = 128
MLA_Q_LORA = 1536
MLA_KV_LORA = 512
MLA_SCALE = (MLA_NOPE + MLA_ROPE) ** -0.5
S5_W = 1024
S5_H = 16
S5_G = S5_W // S5_H
S5_P = 64
S5_LAMBDA_RE_MAX = -1e-4
GQA_Q_HEADS = 8
GQA_KV_HEADS = 2
GQA_HEAD_DIM = 128
GQA_SCALE = GQA_HEAD_DIM ** -0.5

LANES = 128
SUBLANES = 8
VMEM_LIMIT_BYTES = 56 * 1024 * 1024

ZC_CQ = 0
ZC_CKV = 1536
ZC_U = 2048
ZC_GQ = 3072
ZC_GK = 4096
ZC_GV = 4352
ZC_KR = 4608
ZC_KR_ROT = 4736
ZC_GK_ROT = 4864
ZC_GQ_ROT = 5120
Z_WIDTH = 6144

S5_BLK_G = 16
S5_NBLK = S5_G // S5_BLK_G
S5_BLK_IN = S5_BLK_G * S5_H
S5_BLK_ST = S5_BLK_G * S5_P
S5_STATE = S5_G * S5_P
S5_LANE_GRP = 512
S5_TIME_STEPS = 64
S5_SCAN_UNROLL = 2

ATTN_TQ = 256
NORM_ROW_CHUNK = 256
MLA_HEADS_PER_STEP = 4


def _cparams(*sem):
    return pltpu.CompilerParams(dimension_semantics=sem, vmem_limit_bytes=VMEM_LIMIT_BYTES)


def _pick_tile(n, cands=(512, 256, 128, 64, 32, 16, 8)):
    for c in cands:
        if n % c == 0:
            return c
    raise ValueError(f"no tile for {n}")


def _dot(a, b):
    return jnp.dot(a, b, preferred_element_type=F32)


def _dot_nt(a, b):
    return lax.dot_general(a, b, (((1,), (1,)), ((), ())), preferred_element_type=F32)


def _rms(x, g):
    return x * lax.rsqrt(jnp.mean(x * x, axis=-1, keepdims=True) + NORM_EPS) * g


def _normmod(x, g, shift, scale):
    return _rms(x, g) * (1 + scale) + shift


def _mod_body(c_ref, w_ref, b_ref, o_ref):
    a = jax.nn.silu(c_ref[...]).astype(BF16)
    o_ref[0] = _dot(a, w_ref[0].astype(BF16)) + b_ref[0]


def _mod_vectors(c_all, w_mod, b_mod):
    depth, d, n = w_mod.shape
    rows = c_all.shape[0]
    tn = 1024
    return pl.pallas_call(
        _mod_body,
        grid=(depth, n // tn),
        in_specs=[
            pl.BlockSpec((rows, d), lambda l, j: (0, 0)),
            pl.BlockSpec((1, d, tn), lambda l, j: (l, 0, j)),
            pl.BlockSpec((1, 1, tn), lambda l, j: (l, 0, j)),
        ],
        out_specs=pl.BlockSpec((1, rows, tn), lambda l, j: (l, 0, j)),
        out_shape=jax.ShapeDtypeStruct((depth, rows, n), F32),
        compiler_params=_cparams("arbitrary", "arbitrary"),
        name="adaln_mod",
    )(c_all, w_mod, b_mod.reshape(depth, 1, n))


def _ffn_body(x_ref, mod_ref, g_ref, wg_ref, wu_ref, wd_ref, gout_ref, o_ref, h_ref, *, k0, nj, norm_out):
    j = pl.program_id(1)

    @pl.when(j == 0)
    def _():
        h = _normmod(x_ref[...], g_ref[...], mod_ref[0, k0:k0 + 1, :], mod_ref[0, k0 + 1:k0 + 2, :])
        h_ref[...] = h.astype(BF16)
        o_ref[...] = jnp.zeros_like(o_ref)

    h = h_ref[...]
    act = (jax.nn.silu(_dot(h, wg_ref[...])) * _dot(h, wu_ref[...])).astype(BF16)
    o_ref[...] += _dot(act, wd_ref[...])

    @pl.when(j == nj - 1)
    def _():
        y = x_ref[...] + 0.5 * mod_ref[0, k0 + 2:k0 + 3, :] * o_ref[...]
        o_ref[...] = _rms(y, gout_ref[...]) if norm_out else y


def _ffn(x, mod, rows, g, w_up, w_down, k0, g_out=None):
    t, d = x.shape
    dff = w_down.shape[0]
    row_fn, seq = rows
    tm = _pick_tile(seq)
    tf = 512
    nj = dff // tf
    return pl.pallas_call(
        functools.partial(_ffn_body, k0=k0, nj=nj, norm_out=g_out is not None),
        grid=(t // tm, nj),
        in_specs=[
            pl.BlockSpec((tm, d), lambda i, j: (i, 0)),
            pl.BlockSpec((1, N_MOD, d), lambda i, j: (row_fn(i, tm), 0, 0)),
            pl.BlockSpec((1, d), lambda i, j: (0, 0)),
            pl.BlockSpec((d, tf), lambda i, j: (0, j)),
            pl.BlockSpec((d, tf), lambda i, j: (0, j + nj)),
            pl.BlockSpec((tf, d), lambda i, j: (j, 0)),
            pl.BlockSpec((1, d), lambda i, j: (0, 0)),
        ],
        out_specs=pl.BlockSpec((tm, d), lambda i, j: (i, 0)),
        out_shape=jax.ShapeDtypeStruct((t, d), F32),
        scratch_shapes=[pltpu.VMEM((tm, d), BF16)],
        compiler_params=_cparams("parallel", "arbitrary"),
        name="ffn_swiglu",
    )(x, mod, g.reshape(1, d), w_up, w_up, w_down, (g if g_out is None else g_out).reshape(1, d))


def _mixin_body(x_ref, mod_ref, g_ref, w_ref, h_ref, z_ref, *, k0):
    @pl.when(pl.program_id(1) == 0)
    def _():
        tm = x_ref.shape[0]
        for r in range(0, tm, NORM_ROW_CHUNK):
            rows = slice(r, min(r + NORM_ROW_CHUNK, tm))
            h = _normmod(x_ref[rows, :], g_ref[...], mod_ref[0, k0:k0 + 1, :], mod_ref[0, k0 + 1:k0 + 2, :])
            h_ref[rows, :] = h.astype(BF16)

    z_ref[...] = _dot(h_ref[...], w_ref[...])


def _mixer_in(x, mod, rows, g, w_p, k0):
    t, d = x.shape
    n = w_p.shape[1]
    row_fn, seq = rows
    tm = _pick_tile(seq, (1024, 512, 256, 128, 64, 32, 16, 8))
    tn = 1024
    return pl.pallas_call(
        functools.partial(_mixin_body, k0=k0),
        grid=(t // tm, n // tn),
        in_specs=[
            pl.BlockSpec((tm, d), lambda i, j: (i, 0)),
            pl.BlockSpec((1, N_MOD, d), lambda i, j: (row_fn(i, tm), 0, 0)),
            pl.BlockSpec((1, d), lambda i, j: (0, 0)),
            pl.BlockSpec((d, tn), lambda i, j: (0, j)),
        ],
        out_specs=[
            pl.BlockSpec((tm, d), lambda i, j: (i, 0)),
            pl.BlockSpec((tm, tn), lambda i, j: (i, j)),
        ],
        out_shape=[jax.ShapeDtypeStruct((t, d), BF16), jax.ShapeDtypeStruct((t, n), F32)],
        compiler_params=_cparams("parallel", "arbitrary"),
        name="mixer_in",
    )(x, mod, g.reshape(1, d), w_p)


def _mla_proj_body(cq_ref, ckv_ref, kr_ref, krr_ref, cos_ref, sin_ref, gq_ref, gkv_ref,
                   wq_ref, wqr_ref, wkv_ref, q_ref, kv_ref, kro_ref):
    cos = cos_ref[...]
    sin = sin_ref[...]
    nq = _rms(cq_ref[...], gq_ref[...]).astype(BF16)
    qm = _dot(nq, wq_ref[...])
    qr = _dot(nq, wqr_ref[...])
    hw = MLA_NOPE + LANES
    for h in range(MLA_HEADS):
        q_ref[:, h * hw:h * hw + MLA_NOPE] = qm[:, h * hw:h * hw + MLA_NOPE].astype(BF16)
        rope = qm[:, h * hw + MLA_NOPE:(h + 1) * hw] * cos + qr[:, h * LANES:(h + 1) * LANES] * sin
        q_ref[:, h * hw + MLA_NOPE:(h + 1) * hw] = rope.astype(BF16)
    nkv = _rms(ckv_ref[...], gkv_ref[...]).astype(BF16)
    kv_ref[...] = _dot(nkv, wkv_ref[...]).astype(BF16)
    kro_ref[...] = (kr_ref[...] * cos + krr_ref[...] * sin).astype(BF16)


def _mla_proj(z, cos, sin, g_cq, g_ckv, wq, wqr, wkv, tm):
    t = z.shape[0]
    nblk = cos.shape[0] // tm
    hw = MLA_NOPE + LANES
    return pl.pallas_call(
        _mla_proj_body,
        grid=(t // tm,),
        in_specs=[
            pl.BlockSpec((tm, MLA_Q_LORA), lambda i: (i, ZC_CQ // MLA_Q_LORA)),
            pl.BlockSpec((tm, MLA_KV_LORA), lambda i: (i, ZC_CKV // MLA_KV_LORA)),
            pl.BlockSpec((tm, LANES), lambda i: (i, ZC_KR // LANES)),
            pl.BlockSpec((tm, LANES), lambda i: (i, ZC_KR_ROT // LANES)),
            pl.BlockSpec((tm, LANES), lambda i: (i % nblk, 0)),
            pl.BlockSpec((tm, LANES), lambda i: (i % nblk, 0)),
            pl.BlockSpec((1, MLA_Q_LORA), lambda i: (0, 0)),
            pl.BlockSpec((1, MLA_KV_LORA), lambda i: (0, 0)),
            pl.BlockSpec(wq.shape, lambda i: (0, 0)),
            pl.BlockSpec(wqr.shape, lambda i: (0, 0)),
            pl.BlockSpec(wkv.shape, lambda i: (0, 0)),
        ],
        out_specs=[
            pl.BlockSpec((tm, MLA_HEADS * hw), lambda i: (i, 0)),
            pl.BlockSpec((tm, wkv.shape[1]), lambda i: (i, 0)),
            pl.BlockSpec((tm, LANES), lambda i: (i, 0)),
        ],
        out_shape=[
            jax.ShapeDtypeStruct((t, MLA_HEADS * hw), BF16),
            jax.ShapeDtypeStruct((t, wkv.shape[1]), BF16),
            jax.ShapeDtypeStruct((t, LANES), BF16),
        ],
        compiler_params=_cparams("parallel"),
        name="mla_proj",
    )(z, z, z, z, cos, sin, g_cq.reshape(1, -1), g_ckv.reshape(1, -1), wq, wqr, wkv)


def _gqa_prep_body(gq_ref, gqr_ref, gk_ref, gkr_ref, gv_ref, cos_ref, sin_ref,
                   g_ref, q_ref, k_ref, v_ref):
    cos = cos_ref[...]
    sin = sin_ref[...]
    hd = GQA_HEAD_DIM

    def head(x, xr, g, gr):
        r = lax.rsqrt(jnp.mean(x * x, axis=-1, keepdims=True) + NORM_EPS)
        return ((x * r * g) * cos + (xr * r * gr) * sin).astype(BF16)

    for h in range(GQA_Q_HEADS):
        q_ref[:, h * hd:(h + 1) * hd] = head(gq_ref[:, h * hd:(h + 1) * hd], gqr_ref[:, h * hd:(h + 1) * hd],
                                             g_ref[0:1, :], g_ref[1:2, :])
    for h in range(GQA_KV_HEADS):
        k_ref[:, h * hd:(h + 1) * hd] = head(gk_ref[:, h * hd:(h + 1) * hd], gkr_ref[:, h * hd:(h + 1) * hd],
                                             g_ref[2:3, :], g_ref[3:4, :])
    v_ref[...] = gv_ref[...].astype(BF16)


def _gqa_prep(z, cos, sin, gains, tm):
    t = z.shape[0]
    nblk = cos.shape[0] // tm
    qw = GQA_Q_HEADS * GQA_HEAD_DIM
    kw = GQA_KV_HEADS * GQA_HEAD_DIM
    return pl.pallas_call(
        _gqa_prep_body,
        grid=(t // tm,),
        in_specs=[
            pl.BlockSpec((tm, qw), lambda i: (i, ZC_GQ // qw)),
            pl.BlockSpec((tm, qw), lambda i: (i, ZC_GQ_ROT // qw)),
            pl.BlockSpec((tm, kw), lambda i: (i, ZC_GK // kw)),
            pl.BlockSpec((tm, kw), lambda i: (i, ZC_GK_ROT // kw)),
            pl.BlockSpec((tm, kw), lambda i: (i, ZC_GV // kw)),
            pl.BlockSpec((tm, LANES), lambda i: (i % nblk, 0)),
            pl.BlockSpec((tm, LANES), lambda i: (i % nblk, 0)),
            pl.BlockSpec((4, GQA_HEAD_DIM), lambda i: (0, 0)),
        ],
        out_specs=[
            pl.BlockSpec((tm, qw), lambda i: (i, 0)),
            pl.BlockSpec((tm, kw), lambda i: (i, 0)),
            pl.BlockSpec((tm, kw), lambda i: (i, 0)),
        ],
        out_shape=[
            jax.ShapeDtypeStruct((t, qw), BF16),
            jax.ShapeDtypeStruct((t, kw), BF16),
            jax.ShapeDtypeStruct((t, kw), BF16),
        ],
        compiler_params=_cparams("parallel"),
        name="gqa_prep",
    )(z, z, z, z, z, cos, sin, gains)


def _attn_body(*refs, nseg, mla, coef, heads, seg_lens):
    q_ref = refs[0]
    per = 3 if mla else 2
    o_ref, k_scr, v_scr = refs[1 + nseg * per:]
    dq = q_ref.shape[1] // heads
    dv = o_ref.shape[1] // heads

    @pl.when(pl.program_id(2) == 0)
    def _():
        off = 0
        for s in range(nseg):
            seg = refs[1 + s * per:1 + (s + 1) * per]
            rows = slice(off, off + seg_lens[s])
            if mla:
                for g in range(heads):
                    k_scr[g, rows, :MLA_NOPE] = seg[0][:, g * MLA_NOPE:(g + 1) * MLA_NOPE]
                    k_scr[g, rows, MLA_NOPE:] = seg[1][...]
                    v_scr[g, rows, :] = seg[2][:, g * MLA_V:(g + 1) * MLA_V]
            else:
                k_scr[0, rows, :] = seg[0][...]
                v_scr[0, rows, :] = seg[1][...]
            off += seg_lens[s]

    for g in range(heads):
        kg = g if mla else 0
        t = _dot_nt(q_ref[:, g * dq:(g + 1) * dq], k_scr[kg]) * coef
        p = jnp.exp2(t - t.max(axis=-1, keepdims=True))
        l = p.sum(axis=-1, keepdims=True)
        o = _dot(p.astype(BF16), v_scr[kg])
        o_ref[:, g * dv:(g + 1) * dv] = (o / l).astype(BF16)


def _attention(q, segs, batch, mla):
    tq_total = q.shape[0]
    lq = tq_total // batch
    tq = _pick_tile(lq, (ATTN_TQ, 128, 64, 32, 16, 8))
    nq = lq // tq
    if mla:
        heads, dq, dv, nsteps = MLA_HEADS_PER_STEP, MLA_NOPE + LANES, MLA_V, MLA_HEADS // MLA_HEADS_PER_STEP
        scale = MLA_SCALE
    else:
        heads, dq, dv, nsteps = GQA_Q_HEADS // GQA_KV_HEADS, GQA_HEAD_DIM, GQA_HEAD_DIM, GQA_KV_HEADS
        scale = GQA_SCALE
    in_specs = [pl.BlockSpec((tq, heads * dq), lambda b, h, i: (b * nq + i, h))]
    args = [q]
    seg_lens = []
    for seg in segs:
        ls = seg[0].shape[0] // batch
        seg_lens.append(ls)
        if mla:
            kv, kr = seg
            v_blk0 = MLA_HEADS * MLA_NOPE // (heads * MLA_V)
            in_specs += [
                pl.BlockSpec((ls, heads * MLA_NOPE), lambda b, h, i: (b, h)),
                pl.BlockSpec((ls, LANES), lambda b, h, i: (b, 0)),
                pl.BlockSpec((ls, heads * MLA_V), lambda b, h, i: (b, v_blk0 + h)),
            ]
            args += [kv, kr, kv]
        else:
            k, v = seg
            in_specs += [
                pl.BlockSpec((ls, GQA_HEAD_DIM), lambda b, h, i: (b, h)),
                pl.BlockSpec((ls, GQA_HEAD_DIM), lambda b, h, i: (b, h)),
            ]
            args += [k, v]
    ltot = sum(seg_lens)
    kheads = heads if mla else 1
    return pl.pallas_call(
        functools.partial(_attn_body, nseg=len(segs), mla=mla, coef=scale * math.log2(math.e), heads=heads,
                          seg_lens=tuple(seg_lens)),
        grid=(batch, nsteps, nq),
        in_specs=in_specs,
        out_specs=pl.BlockSpec((tq, heads * dv), lambda b, h, i: (b * nq + i, h)),
        out_shape=jax.ShapeDtypeStruct((tq_total, nsteps * heads * dv), BF16),
        scratch_shapes=[pltpu.VMEM((kheads, ltot, dq), BF16), pltpu.VMEM((kheads, ltot, dv), BF16)],
        compiler_params=_cparams("parallel", "parallel", "arbitrary"),
        name="attn_mla" if mla else "attn_gqa",
    )(*args)


def _s5_scan_body(u_ref, bre_ref, bim_ref, cre_ref, cim_ref, ar_ref, ai_ref, y_ref, sre, sim, car_r, car_i,
                  *, reverse, ts):
    @pl.when(pl.program_id(0) == 0)
    def _():
        car_r[...] = jnp.zeros_like(car_r)
        car_i[...] = jnp.zeros_like(car_i)

    ub = u_ref[...].astype(BF16)
    for k in range(S5_NBLK):
        st = slice(k * S5_BLK_ST, (k + 1) * S5_BLK_ST)
        uk = ub[:, k * S5_BLK_IN:(k + 1) * S5_BLK_IN]
        sre[:, st] = _dot(uk, bre_ref[k])
        sim[:, st] = _dot(uk, bim_ref[k])

        for lg in range(S5_BLK_ST // S5_LANE_GRP):
            c0 = k * S5_BLK_ST + lg * S5_LANE_GRP
            cols = slice(c0, c0 + S5_LANE_GRP)
            ar = ar_ref[:, cols]
            ai = ai_ref[:, cols]
            sr = car_r[:, cols]
            si = car_i[:, cols]
            for t in (range(ts - 1, -1, -1) if reverse else range(ts)):
                rows = slice(t * SUBLANES, (t + 1) * SUBLANES)
                sr, si = ar * sr - ai * si + sre[rows, cols], ar * si + ai * sr + sim[rows, cols]
                sre[rows, cols] = sr
                sim[rows, cols] = si
            car_r[:, cols] = sr
            car_i[:, cols] = si

        y_ref[:, k * S5_BLK_IN:(k + 1) * S5_BLK_IN] = (
            _dot(sre[:, st].astype(BF16), cre_ref[k]) - _dot(sim[:, st].astype(BF16), cim_ref[k]))


def _s5_scan(u_tb, prm, reverse, n_ctx_steps, ts):
    rows = u_tb.shape[0]
    rt = ts * SUBLANES
    nchunk = rows // rt
    nctx = n_ctx_steps // ts
    if reverse:
        blk = lambda c: jnp.where(c < nctx, nctx - 1 - c, nchunk - 1 - (c - nctx))
    else:
        blk = lambda c: c
    full = lambda a: pl.BlockSpec(a.shape, lambda c: (0,) * a.ndim)
    bre, bim, cre, cim, ar, ai = prm
    return pl.pallas_call(
        functools.partial(_s5_scan_body, reverse=reverse, ts=ts),
        grid=(nchunk,),
        in_specs=[pl.BlockSpec((rt, S5_W), lambda c: (blk(c), 0)),
                  full(bre), full(bim), full(cre), full(cim), full(ar), full(ai)],
        out_specs=pl.BlockSpec((rt, S5_W), lambda c: (blk(c), 0)),
        out_shape=jax.ShapeDtypeStruct((rows, S5_W), F32),
        scratch_shapes=[
            pltpu.VMEM((rt, S5_STATE), F32), pltpu.VMEM((rt, S5_STATE), F32),
            pltpu.VMEM((SUBLANES, S5_STATE), F32), pltpu.VMEM((SUBLANES, S5_STATE), F32),
        ],
        compiler_params=_cparams("arbitrary"),
        name="s5_scan_bwd" if reverse else "s5_scan_fwd",
    )(u_tb, bre, bim, cre, cim, ar, ai)


def _s5_glu_body(u_ref, yf_ref, yb_ref, d_ref, w_ref, b_ref, o_ref):
    y = u_ref[...] * d_ref[...] + yf_ref[...] + yb_ref[...]
    gate = jax.nn.sigmoid(_dot(jax.nn.gelu(y).astype(BF16), w_ref[...]) + b_ref[...])
    o_ref[...] = (y * gate).astype(BF16)


def _s5_glu(u, yf, yb, d_skip, w_glu, b_glu):
    t = u.shape[0]
    tm = _pick_tile(t, (1024, 512, 256, 128, 64, 32, 16, 8))
    row = lambda i: (i, 0)
    const = lambda i: (0, 0)
    return pl.pallas_call(
        _s5_glu_body,
        grid=(t // tm,),
        in_specs=[
            pl.BlockSpec((tm, S5_W), row), pl.BlockSpec((tm, S5_W), row), pl.BlockSpec((tm, S5_W), row),
            pl.BlockSpec((1, S5_W), const), pl.BlockSpec((S5_W, S5_W), const), pl.BlockSpec((1, S5_W), const),
        ],
        out_specs=pl.BlockSpec((tm, S5_W), row),
        out_shape=jax.ShapeDtypeStruct((t, S5_W), BF16),
        compiler_params=_cparams("parallel"),
        name="s5_glu",
    )(u, yf, yb, d_skip, w_glu, b_glu)


def _merge_body(h_ref, b0_ref, b1_ref, b2_ref, x_ref, mod_ref, wg_ref, bg_ref, wb_ref, wo_ref, o_ref, *, k0, nj):
    j = pl.program_id(1)

    @pl.when(j == 0)
    def _():
        o_ref[...] = jnp.zeros_like(o_ref)

    h = h_ref[...]
    y = None
    for n, br in enumerate((b0_ref, b1_ref, b2_ref)):
        gate = jax.nn.sigmoid(_dot(h, wg_ref[n]) + bg_ref[n])
        term = gate * _dot(br[...], wb_ref[n])
        y = term if y is None else y + term
    o_ref[...] += _dot(y.astype(BF16), wo_ref[...])

    @pl.when(j == nj - 1)
    def _():
        o_ref[...] = x_ref[...] + mod_ref[0, k0:k0 + 1, :] * o_ref[...]


def _merge(h, branches, x, mod, rows, w_gate, b_gate, w_branch, w_o, k0):
    t, d = x.shape
    bw = w_branch.shape[1]
    row_fn, seq = rows
    tm = _pick_tile(seq)
    tj = 512
    nj = d // tj
    return pl.pallas_call(
        functools.partial(_merge_body, k0=k0, nj=nj),
        grid=(t // tm, nj),
        in_specs=[
            pl.BlockSpec((tm, d), lambda i, j: (i, 0)),
            pl.BlockSpec((tm, bw), lambda i, j: (i, 0)),
            pl.BlockSpec((tm, bw), lambda i, j: (i, 0)),
            pl.BlockSpec((tm, bw), lambda i, j: (i, 0)),
            pl.BlockSpec((tm, d), lambda i, j: (i, 0)),
            pl.BlockSpec((1, N_MOD, d), lambda i, j: (row_fn(i, tm), 0, 0)),
            pl.BlockSpec((3, d, tj), lambda i, j: (0, 0, j)),
            pl.BlockSpec((3, 1, tj), lambda i, j: (0, 0, j)),
            pl.BlockSpec((3, bw, tj), lambda i, j: (0, 0, j)),
            pl.BlockSpec((tj, d), lambda i, j: (j, 0)),
        ],
        out_specs=pl.BlockSpec((tm, d), lambda i, j: (i, 0)),
        out_shape=jax.ShapeDtypeStruct((t, d), F32),
        compiler_params=_cparams("parallel", "arbitrary"),
        name="gated_merge",
    )(h, *branches, x, mod, w_gate, b_gate, w_branch, w_o)


def _rot_half(v):
    v1, v2 = jnp.split(v, 2, axis=-1)
    return jnp.concatenate([-v2, v1], axis=-1)


def _rot_axial(v):
    a, b = jnp.split(v, 2, axis=-1)
    return jnp.concatenate([_rot_half(a), _rot_half(b)], axis=-1)


def _swap_half(v):
    v1, v2 = jnp.split(v, 2, axis=-1)
    return jnp.concatenate([v2, v1], axis=-1)


def _swap_axial(v):
    a, b = jnp.split(v, 2, axis=-1)
    return jnp.concatenate([_swap_half(a), _swap_half(b)], axis=-1)


def _rope_tables(n_tokens, d_rot):
    rows = n_tokens // GRID_W
    r, col = jnp.meshgrid(jnp.arange(rows, dtype=jnp.int32), jnp.arange(GRID_W, dtype=jnp.int32), indexing='ij')
    half = d_rot // 2
    freqs = ROPE_THETA ** (-jnp.arange(0, half, 2, dtype=F32) / half)

    def table(pos):
        ang = pos.reshape(-1).astype(F32)[:, None] * freqs[None, :]
        ang = jnp.concatenate([ang, ang], axis=-1)
        return jnp.cos(ang), jnp.sin(ang)

    cr, sr = table(r)
    cc, sc = table(col)
    return jnp.concatenate([cr, cc], axis=-1), jnp.concatenate([sr, sc], axis=-1)


def _pad_lanes(a, width):
    return jnp.pad(a, [(0, 0)] * (a.ndim - 1) + [(0, width - a.shape[-1])])


def _prep_w_in(w):
    d = w.shape[0]
    offs = [0, MLA_Q_LORA, MLA_KV_LORA, MLA_ROPE, S5_W, GQA_Q_HEADS * GQA_HEAD_DIM,
            GQA_KV_HEADS * GQA_HEAD_DIM, GQA_KV_HEADS * GQA_HEAD_DIM]
    offs = [sum(offs[:i + 1]) for i in range(len(offs))]
    cq, ckv, kr, u, gq, gk, gv = [w[:, offs[i]:offs[i + 1]] for i in range(7)]
    rot_heads = lambda a: _rot_axial(a.reshape(d, -1, GQA_HEAD_DIM)).reshape(d, -1)
    cols = [cq, ckv, u, gq, gk, gv, _pad_lanes(kr, LANES), _pad_lanes(_rot_axial(kr), LANES),
            rot_heads(gk), rot_heads(gq)]
    wp = jnp.concatenate(cols, axis=1)
    assert wp.shape[1] == Z_WIDTH
    return wp.astype(BF16)


def _prep_mla(w_uq, w_ukv):
    r = w_uq.shape[0]
    wq = w_uq.reshape(r, MLA_HEADS, MLA_NOPE + MLA_ROPE)
    nope, rope = wq[..., :MLA_NOPE], wq[..., MLA_NOPE:]
    wq_main = jnp.concatenate([nope, _pad_lanes(rope, LANES)], axis=-1).reshape(r, -1)
    wq_rot = _pad_lanes(_rot_axial(rope), LANES).reshape(r, -1)
    rk = w_ukv.shape[0]
    wkv = w_ukv.reshape(rk, MLA_HEADS, MLA_NOPE + MLA_V)
    wkv = jnp.concatenate([wkv[..., :MLA_NOPE].reshape(rk, -1), wkv[..., MLA_NOPE:].reshape(rk, -1)], axis=1)
    return wq_main.astype(BF16), wq_rot.astype(BF16), wkv.astype(BF16)


def _s5_discretize(lam_re, lam_im, log_dt, b_re, b_im):
    lr = jnp.minimum(lam_re.astype(F32), S5_LAMBDA_RE_MAX)
    li = lam_im.astype(F32)
    dt = jnp.exp(log_dt.astype(F32))[:, None]
    mag = jnp.exp(lr * dt)
    a_r, a_i = mag * jnp.cos(li * dt), mag * jnp.sin(li * dt)
    den = lr * lr + li * li
    n_r, n_i = a_r - 1.0, a_i
    f_r = (n_r * lr + n_i * li) / den
    f_i = (n_i * lr - n_r * li) / den
    br, bi = b_re.astype(F32), b_im.astype(F32)
    bb_r = f_r[..., None] * br - f_i[..., None] * bi
    bb_i = f_r[..., None] * bi + f_i[..., None] * br
    return a_r, a_i, bb_r, bb_i


def _block_diag(w):
    g, a, b = w.shape
    w4 = w.reshape(S5_NBLK, S5_BLK_G, a, b)
    eye = jnp.eye(S5_BLK_G, dtype=w.dtype)
    out = jnp.einsum('kgab,gh->kgahb', w4, eye)
    return out.reshape(S5_NBLK, S5_BLK_G * a, S5_BLK_G * b)


def _prep_s5(lam_re, lam_im, log_dt, b_re, b_im, c_re, c_im):
    a_r, a_i, bb_r, bb_i = _s5_discretize(lam_re, lam_im, log_dt, b_re, b_im)
    bre = _block_diag(jnp.swapaxes(bb_r, 1, 2)).astype(BF16)
    bim = _block_diag(jnp.swapaxes(bb_i, 1, 2)).astype(BF16)
    cre = _block_diag(jnp.swapaxes(c_re.astype(F32), 1, 2)).astype(BF16)
    cim = _block_diag(jnp.swapaxes(c_im.astype(F32), 1, 2)).astype(BF16)
    ar = jnp.broadcast_to(a_r.reshape(1, -1), (SUBLANES, S5_STATE))
    ai = jnp.broadcast_to(a_i.reshape(1, -1), (SUBLANES, S5_STATE))
    return bre, bim, cre, cim, ar, ai


def kernel(x, c, ctx, c_ctx, w_mod, b_mod, norm_g, w_ffn_up, w_ffn_down, w_in, mla_g_cq, mla_g_ckv, mla_w_uq, mla_w_ukv, gqa_g_q, gqa_g_k, s5_lam_re, s5_lam_im, s5_log_dt, s5_b_re, s5_b_im, s5_c_re, s5_c_im, s5_d, s5_w_glu, s5_b_glu, w_gate, b_gate, w_branch, w_o, final_g):
    b, l, d = x.shape
    lc = ctx.shape[1]
    depth = w_mod.shape[0]

    n_rows = -(-(b + 1) // SUBLANES) * SUBLANES
    c_all = jnp.concatenate([c, c_ctx[None, :], jnp.zeros((n_rows - b - 1, d), F32)], axis=0)
    mod_all = _mod_vectors(c_all, w_mod, b_mod).reshape(depth, n_rows, N_MOD, d)
    lat_row = (lambda i, tm: (i * tm) // l, l)
    ctx_row = (lambda i, tm: b, b * lc)

    assert b == SUBLANES, "the S5 scan lays the batch on the 8 sublanes"
    tp_lat = _pick_tile(l)
    tp_ctx = _pick_tile(lc, (256, 128, 64, 32, 16, 8))
    ts5 = _pick_tile(math.gcd(l, lc), (S5_TIME_STEPS, 32, 16, 8))
    cos_m, sin_m = _rope_tables(l, MLA_ROPE)
    cos_m, sin_m = _pad_lanes(cos_m, LANES), _pad_lanes(sin_m, LANES)
    cos_g, sin_g = _rope_tables(l, GQA_HEAD_DIM)
    ones_t, zeros_t = jnp.ones((tp_ctx, LANES), F32), jnp.zeros((tp_ctx, LANES), F32)

    x_lat = x.reshape(b * l, d)
    x_ctx = ctx.reshape(b * lc, d)
    for li in range(depth):
        last = li == depth - 1
        mod = mod_all[li]
        g = norm_g[li]
        up = [w_ffn_up[li, f].astype(BF16) for f in range(2)]
        down = [w_ffn_down[li, f].astype(BF16) for f in range(2)]

        x_lat = _ffn(x_lat, mod, lat_row, g[0], up[0], down[0], 0)
        x_ctx = _ffn(x_ctx, mod, ctx_row, g[0], up[0], down[0], 0)

        w_p = _prep_w_in(w_in[li])
        h_lat, z_lat = _mixer_in(x_lat, mod, lat_row, g[1], w_p, 3)
        h_ctx, z_ctx = _mixer_in(x_ctx, mod, ctx_row, g[1], w_p, 3)

        wq, wqr, wkv = _prep_mla(mla_w_uq[li], mla_w_ukv[li])
        q_l, kv_l, kr_l = _mla_proj(z_lat, cos_m, sin_m, mla_g_cq[li], mla_g_ckv[li], wq, wqr, wkv, tp_lat)
        q_c, kv_c, kr_c = _mla_proj(z_ctx, ones_t, zeros_t, mla_g_cq[li], mla_g_ckv[li], wq, wqr, wkv, tp_ctx)
        gains = jnp.stack([gqa_g_q[li], _swap_axial(gqa_g_q[li]), gqa_g_k[li], _swap_axial(gqa_g_k[li])]).astype(F32)
        gq_l, gk_l, gv_l = _gqa_prep(z_lat, cos_g, sin_g, gains, tp_lat)
        gq_c, gk_c, gv_c = _gqa_prep(z_ctx, ones_t, zeros_t, gains, tp_ctx)

        mla_l = _attention(q_l, [(kv_c, kr_c), (kv_l, kr_l)], b, True)
        gqa_l = _attention(gq_l, [(gk_c, gv_c), (gk_l, gv_l)], b, False)

        u_tb = jnp.concatenate([
            jnp.swapaxes(z_ctx[:, ZC_U:ZC_U + S5_W].reshape(b, lc, S5_W), 0, 1),
            jnp.swapaxes(z_lat[:, ZC_U:ZC_U + S5_W].reshape(b, l, S5_W), 0, 1)], axis=0).reshape((lc + l) * b, S5_W)
        ys = []
        for direction in range(2):
            prm = _prep_s5(s5_lam_re[li, direction], s5_lam_im[li, direction], s5_log_dt[li, direction],
                           s5_b_re[li, direction], s5_b_im[li, direction],
                           s5_c_re[li, direction], s5_c_im[li, direction])
            ys.append(_s5_scan(u_tb, prm, direction == 1, lc, ts5))
        d_skip = s5_d[li].reshape(1, S5_W).astype(F32)
        w_glu = s5_w_glu[li].astype(BF16)
        b_glu = s5_b_glu[li].reshape(1, S5_W).astype(F32)
        s5_tb = _s5_glu(u_tb, ys[0], ys[1], d_skip, w_glu, b_glu).reshape(lc + l, b, S5_W)
        s5_l = jnp.swapaxes(s5_tb[lc:], 0, 1).reshape(b * l, S5_W)

        wg = w_gate[li].astype(BF16)
        bg = b_gate[li].reshape(3, 1, d).astype(F32)
        wb = w_branch[li].astype(BF16)
        wo = w_o[li].astype(BF16)
        x_lat_new = _merge(h_lat, (mla_l, s5_l, gqa_l), x_lat, mod, lat_row, wg, bg, wb, wo, 5)
        if not last:
            mla_c = _attention(q_c, [(kv_c, kr_c)], b, True)
            gqa_c = _attention(gq_c, [(gk_c, gv_c)], b, False)
            s5_c = jnp.swapaxes(s5_tb[:lc], 0, 1).reshape(b * lc, S5_W)
            x_ctx = _merge(h_ctx, (mla_c, s5_c, gqa_c), x_ctx, mod, ctx_row, wg, bg, wb, wo, 5)
            x_ctx = _ffn(x_ctx, mod, ctx_row, g[2], up[1], down[1], 6)
        x_lat = _ffn(x_lat_new, mod, lat_row, g[2], up[1], down[1], 6, final_g if last else None)

    return x_lat.reshape(b, l, d)
```
